```python
import jax, jax.numpy as jnp
from jax import lax
import numpy as np

D_MODEL = 2048
BATCH = 4
SEQ = 4096
DEPTH = 2

CHUNK = 64
Q_BLOCK = 128
N_BRANCHES = 4
BRANCH_WIDTH = D_MODEL // N_BRANCHES
POOL_WINDOWS = (2, 4, 8, 16)
POOL_GROUP = BRANCH_WIDTH // len(POOL_WINDOWS)
DSA_HEADS = 4
DSA_HEAD_DIM = BRANCH_WIDTH // DSA_HEADS
IDX_HEADS = 16
IDX_DIM = 64
TOPK_MAX = 256
MLA_HEADS = 4
MLA_NOPE = 128
MLA_ROPE = 64
MLA_V = BRANCH_WIDTH // MLA_HEADS
Q_LORA = 384
KV_LORA = 256
ROPE_BASE = 10000.0
CONV_WIDTH = 31
CONV_CH = BRANCH_WIDTH
FFN_HIDDEN = -(-8 * D_MODEL // (3 * 256)) * 256

LN_EPS = 1e-5
DEEPNORM_ALPHA = (2 * DEPTH) ** 0.25
DEEPNORM_BETA = (8 * DEPTH) ** -0.25

IN_SIZES = (
    BRANCH_WIDTH,
    DSA_HEADS * DSA_HEAD_DIM,
    DSA_HEADS * DSA_HEAD_DIM,
    DSA_HEADS * DSA_HEAD_DIM,
    IDX_HEADS * IDX_DIM,
    IDX_DIM,
    IDX_HEADS,
    Q_LORA,
    KV_LORA,
    MLA_ROPE,
    2 * CONV_CH,
)
IN_WIDTH = sum(IN_SIZES)
IN_SPLITS = tuple(int(v) for v in np.cumsum(IN_SIZES)[:-1])

kernel_name = 'hybrid_streaming_encoder_block'


def layer_norm(x, g, b):
    xf = x.astype(jnp.float32)
    mu = xf.mean(-1, keepdims=True)
    var = jnp.square(xf - mu).mean(-1, keepdims=True)
    return ((xf - mu) * lax.rsqrt(var + LN_EPS) * g + b).astype(x.dtype)


def plain_norm(x):
    xf = x.astype(jnp.float32)
    mu = xf.mean(-1, keepdims=True)
    var = jnp.square(xf - mu).mean(-1, keepdims=True)
    return ((xf - mu) * lax.rsqrt(var + LN_EPS)).astype(x.dtype)


def rms_norm(x, g):
    xf = x.astype(jnp.float32)
    return (xf * lax.rsqrt(jnp.mean(xf * xf, -1, keepdims=True) + LN_EPS) * g).astype(x.dtype)


def apply_rope(x, cos, sin):
    x1, x2 = jnp.split(x.astype(jnp.float32), 2, axis=-1)
    return jnp.concatenate([x1 * cos - x2 * sin, x2 * cos + x1 * sin], axis=-1).astype(x.dtype)


def alibi_slopes(n_heads):
    return jnp.asarray([2.0 ** (-8.0 * (h + 1) / n_heads) for h in range(n_heads)], jnp.float32)


def to_blocks(a):
    b, s = a.shape[:2]
    return jnp.swapaxes(a.reshape((b, s // Q_BLOCK, Q_BLOCK) + a.shape[2:]), 0, 1)


def from_blocks(o):
    nb, b, qb = o.shape[:3]
    return jnp.swapaxes(o, 0, 1).reshape(b, nb * qb, -1)


def pool_mixer(h, w_pool, pool_scale):
    b, s, _ = h.shape
    hf = h.astype(jnp.float32).reshape(b, s, len(POOL_WINDOWS), POOL_GROUP)
    cs = jnp.concatenate([jnp.zeros((b, 1) + hf.shape[2:], jnp.float32),
                          jnp.cumsum(hf, axis=1)], axis=1)
    t = jnp.arange(s)
    outs = []
    for g, w in enumerate(POOL_WINDOWS):
        cs_g = cs[:, :, g]
        lo = jnp.maximum(t + 1 - w, 0)
        cnt = jnp.minimum(t + 1, w).astype(jnp.float32)[None, :, None]
        outs.append((cs_g[:, t + 1] - cs_g[:, lo]) / cnt - hf[:, :, g])
    pooled = jnp.stack(outs, axis=2).astype(h.dtype)
    mixed = jnp.einsum('bsgc,gcd->bsgd', pooled, w_pool)
    return mixed.reshape(b, s, BRANCH_WIDTH) * pool_scale


def dsa_mixer(q, k, v, iq, ik, iw):
    b, s, n_heads, dh = q.shape
    n_sel = min(TOPK_MAX, s // 4)
    slopes = alibi_slopes(n_heads)
    key_pos = jnp.arange(s)
    ik_f = ik.astype(jnp.float32)

    def block(args):
        qb, iqb, iwb, t = args
        limit = (t // CHUNK + 1) * CHUNK
        admissible = key_pos[None, :] < limit[:, None]
        dots = jnp.einsum('bqhd,bsd->bqhs', iqb.astype(jnp.float32), ik_f) * IDX_DIM ** -0.5
        score = jnp.einsum('bqh,bqhs->bqs', iwb.astype(jnp.float32) * IDX_HEADS ** -0.5,
                           jax.nn.relu(dots))
        score = jnp.where(admissible[None], score, -jnp.inf)
        _, idx = lax.top_k(score, n_sel)
        valid = idx < limit[None, :, None]
        k_sel = jax.vmap(lambda kk, ii: kk[ii])(k, idx)
        v_sel = jax.vmap(lambda vv, ii: vv[ii])(v, idx)
        logits = jnp.einsum('bqhd,bqkhd->bhqk', qb, k_sel).astype(jnp.float32) * dh ** -0.5
        dist = jnp.abs(t[None, :, None] - idx).astype(jnp.float32)
        logits = logits - slopes[None, :, None, None] * dist[:, None]
        logits = jnp.where(valid[:, None], logits, -jnp.inf)
        p = jax.nn.softmax(logits, axis=-1).astype(v.dtype)
        return jnp.einsum('bhqk,bqkhd->bqhd', p, v_sel)

    t_blocks = jnp.arange(s).reshape(s // Q_BLOCK, Q_BLOCK)
    out = lax.map(block, (to_blocks(q), to_blocks(iq), to_blocks(iw), t_blocks))
    return from_blocks(out)


def mla_mixer(cq, ckv, kr, q_norm, w_q_up, kv_norm, w_kv_up, cos, sin):
    b, s, _ = cq.shape
    q = (rms_norm(cq, q_norm) @ w_q_up).reshape(b, s, MLA_HEADS, MLA_NOPE + MLA_ROPE)
    q_nope = q[..., :MLA_NOPE]
    q_rope = apply_rope(q[..., MLA_NOPE:], cos[:, None], sin[:, None])
    kv = (rms_norm(ckv, kv_norm) @ w_kv_up).reshape(b, s, MLA_HEADS, MLA_NOPE + MLA_V)
    k_nope, v = kv[..., :MLA_NOPE], kv[..., MLA_NOPE:]
    k_rope = apply_rope(kr, cos, sin)
    scale = (MLA_NOPE + MLA_ROPE) ** -0.5
    key_chunk = jnp.arange(s) // CHUNK

    def block(args):
        qn, qr, t = args
        logits = (jnp.einsum('bqhd,bshd->bhqs', qn, k_nope).astype(jnp.float32)
                  + jnp.einsum('bqhr,bsr->bhqs', qr, k_rope).astype(jnp.float32)) * scale
        mask = key_chunk[None, :] <= (t // CHUNK)[:, None]
        logits = jnp.where(mask, logits, -jnp.inf)
        p = jax.nn.softmax(logits, axis=-1).astype(v.dtype)
        return jnp.einsum('bhqs,bshd->bqhd', p, v)

    t_blocks = jnp.arange(s).reshape(s // Q_BLOCK, Q_BLOCK)
    out = lax.map(block, (to_blocks(q_nope), to_blocks(q_rope), t_blocks))
    return from_blocks(out)


def conv_mixer(h, w_dw, b_dw, ln_g, ln_b):
    a, g = jnp.split(h, 2, axis=-1)
    z = a * jax.nn.sigmoid(g)
    z = lax.conv_general_dilated(z, w_dw[:, None, :], (1,), [(CONV_WIDTH - 1, 0)],
                                 dimension_numbers=('NWC', 'WIO', 'NWC'),
                                 feature_group_count=CONV_CH) + b_dw
    return jax.nn.silu(layer_norm(z, ln_g, ln_b))


def setup_inputs(seed: int = 0) -> dict:
    key = jax.random.key(seed)
    keys = list(jax.random.split(key, 32))
    L, D = DEPTH, D_MODEL

    def nrm(shape, fan_in, scale=1.0):
        return jax.random.normal(keys.pop(), shape, jnp.float32) * (scale * fan_in ** -0.5)

    def gain(shape):
        return 1.0 + 0.01 * jax.random.normal(keys.pop(), shape, jnp.float32)

    def bias(shape):
        return 0.01 * jax.random.normal(keys.pop(), shape, jnp.float32)

    return {
        'x': jax.random.normal(keys.pop(), (BATCH, SEQ, D), jnp.float32),
        'c': jax.random.normal(keys.pop(), (BATCH, D), jnp.float32),
        'w_ada': nrm((L, D, 6 * D), D, 0.2),
        'b_ada': bias((L, 6 * D)),
        'w_in': nrm((L, D, IN_WIDTH), D),
        'w_pool': nrm((L, len(POOL_WINDOWS), POOL_GROUP, POOL_GROUP), POOL_GROUP),
        'pool_scale': gain((L, BRANCH_WIDTH)),
        'q_norm': gain((L, Q_LORA)),
        'w_q_up': nrm((L, Q_LORA, MLA_HEADS * (MLA_NOPE + MLA_ROPE)), Q_LORA),
        'kv_norm': gain((L, KV_LORA)),
        'w_kv_up': nrm((L, KV_LORA, MLA_HEADS * (MLA_NOPE + MLA_V)), KV_LORA),
        'w_dw': nrm((L, CONV_WIDTH, CONV_CH), CONV_WIDTH),
        'b_dw': bias((L, CONV_CH)),
        'conv_ln_g': gain((L, CONV_CH)),
        'conv_ln_b': bias((L, CONV_CH)),
        'w_branch': nrm((L, N_BRANCHES, BRANCH_WIDTH, D), BRANCH_WIDTH, DEEPNORM_BETA),
        'w_gate': nrm((L, N_BRANCHES, D, D), D),
        'b_gate': bias((L, N_BRANCHES, D)),
        'w_o': nrm((L, D, D), D, DEEPNORM_BETA),
        'ln1_g': gain((L, D)),
        'ln1_b': bias((L, D)),
        'w_ffn_in': nrm((L, D, 2 * FFN_HIDDEN), D),
        'w_ffn_out': nrm((L, FFN_HIDDEN, D), FFN_HIDDEN, DEEPNORM_BETA),
        'ln2_g': gain((L, D)),
        'ln2_b': bias((L, D)),
    }


def reference(x, c, w_ada, b_ada, w_in, w_pool, pool_scale, q_norm, w_q_up, kv_norm,
              w_kv_up, w_dw, b_dw, conv_ln_g, conv_ln_b, w_branch, w_gate, b_gate, w_o,
              ln1_g, ln1_b, w_ffn_in, w_ffn_out, ln2_g, ln2_b):
    b, s, d = x.shape
    pos = jnp.arange(s, dtype=jnp.float32)
    inv_freq = ROPE_BASE ** (-jnp.arange(0, MLA_ROPE, 2, dtype=jnp.float32) / MLA_ROPE)
    ang = pos[:, None] * inv_freq[None, :]
    cos, sin = jnp.cos(ang), jnp.sin(ang)
    c_act = jax.nn.silu(c)

    for l in range(DEPTH):
        mod = (c_act @ w_ada[l] + b_ada[l]).reshape(b, 6, 1, d)
        sh1, sc1, g1, sh2, sc2, g2 = (mod[:, i] for i in range(6))

        u = plain_norm(x) * (1.0 + sc1) + sh1
        (h_pool, dq, dk, dv, iq, ik, iw, cq, ckv, kr, h_conv) = jnp.split(
            u @ w_in[l], IN_SPLITS, axis=-1)
        y_a = pool_mixer(h_pool, w_pool[l], pool_scale[l])
        y_b = dsa_mixer(dq.reshape(b, s, DSA_HEADS, DSA_HEAD_DIM),
                        dk.reshape(b, s, DSA_HEADS, DSA_HEAD_DIM),
                        dv.reshape(b, s, DSA_HEADS, DSA_HEAD_DIM),
                        iq.reshape(b, s, IDX_HEADS, IDX_DIM), ik, iw)
        y_c = mla_mixer(cq, ckv, kr, q_norm[l], w_q_up[l], kv_norm[l], w_kv_up[l], cos, sin)
        y_d = conv_mixer(h_conv, w_dw[l], b_dw[l], conv_ln_g[l], conv_ln_b[l])
        gated = [jax.nn.sigmoid(u @ w_gate[l, i] + b_gate[l, i]) * (y @ w_branch[l, i])
                 for i, y in enumerate((y_a, y_b, y_c, y_d))]
        merged = gated[0] + gated[1] + gated[2] + gated[3]
        x = layer_norm(DEEPNORM_ALPHA * x + (1.0 + g1) * (merged @ w_o[l]), ln1_g[l], ln1_b[l])

        u2 = plain_norm(x) * (1.0 + sc2) + sh2
        a, gt = jnp.split(u2 @ w_ffn_in[l], 2, axis=-1)
        ffn = (jax.nn.silu(a) * gt) @ w_ffn_out[l]
        x = layer_norm(DEEPNORM_ALPHA * x + (1.0 + g2) * ffn, ln2_g[l], ln2_b[l])
    return x
```

```python
import functools

import jax
import jax.numpy as jnp
from jax import lax
from jax.experimental import pallas as pl
from jax.experimental.pallas import tpu as pltpu

F32 = jnp.float32
BF16 = jnp.bfloat16
I32 = jnp.int32

D_MODEL = 2048
DEPTH = 2
CHUNK = 64
BRANCH = 512
POOL_WINDOWS = (2, 4, 8, 16)
POOL_GROUP = 128
POOL_HALO = 16
DSA_HEADS = 4
DSA_DIM = 128
IDX_HEADS = 16
IDX_DIM = 64
TOPK_MAX = 256
MLA_HEADS = 4
MLA_NOPE = 128
MLA_ROPE = 64
MLA_V = 128
MLA_QK_PAD = 256
Q_LORA = 384
KV_LORA = 256
ROPE_BASE = 10000.0
CONV_WIDTH = 31
CONV_HALO = 32
SUBLANES = 8
FFN_HIDDEN = 5632
LN_EPS = 1e-5
ALPHA = (2 * DEPTH) ** 0.25

OFF_POOL, OFF_DQ, OFF_DK, OFF_DV, OFF_IQ, OFF_IKW = 0, 512, 1024, 1536, 2048, 3072
OFF_CQ, OFF_CKV, OFF_KR, OFF_CONV, IN_PAD = 3200, 3584, 3840, 3968, 4992

TOKEN_TILE = 512
KEY_CHUNK = 512
Q_TILE = 256
VMEM_LIMIT = 56 * 1024 * 1024
NEG = -1e30
M_INIT = -1e29
LOG2E = 1.4426950408889634
COUNT_ROWS = 32
SEARCH_FIRST_STEPS = 16
VALUE_BISECT_STEPS = 40
SEARCH_MAX_STEPS = 104
DSA_SLOPES = tuple(2.0 ** (-8.0 * (h + 1) / DSA_HEADS) for h in range(DSA_HEADS))


def _nt(a, b):
    return lax.dot_general(a, b, (((1,), (1,)), ((), ())), preferred_element_type=F32)


def _mm(a, b):
    return jnp.dot(a, b, preferred_element_type=F32)


def _plain_norm(x):
    mu = jnp.mean(x, axis=-1, keepdims=True)
    xc = x - mu
    var = jnp.mean(xc * xc, axis=-1, keepdims=True)
    return xc * lax.rsqrt(var + LN_EPS)


def _silu(x):
    return x * jax.nn.sigmoid(x)


def _ada_body(c_ref, w_ref, b_ref, o_ref):
    act = _silu(c_ref[...])
    o_ref[0] = _mm(act.astype(BF16), w_ref[0].astype(BF16)) + b_ref[0]


def _ada(c_pad, w_ada, b_ada):
    depth, d, n = w_ada.shape
    tn = 1024
    return pl.pallas_call(
        _ada_body,
        grid=(depth, n // tn),
        in_specs=[
            pl.BlockSpec((8, d), lambda l, j: (0, 0)),
            pl.BlockSpec((1, d, tn), lambda l, j: (l, 0, j)),
            pl.BlockSpec((1, 1, tn), lambda l, j: (l, 0, j)),
        ],
        out_specs=pl.BlockSpec((1, 8, tn), lambda l, j: (l, 0, j)),
        out_shape=jax.ShapeDtypeStruct((depth, 8, n), F32),
        compiler_params=pltpu.CompilerParams(
            dimension_semantics=("arbitrary", "arbitrary"), vmem_limit_bytes=VMEM_LIMIT),
    )(c_pad, w_ada, b_ada.reshape(depth, 1, n))


def _rope128(r, tab_ref):
    return (r * tab_ref[0] + pltpu.roll(r, 32, 1) * tab_ref[1]
            + pltpu.roll(r, 96, 1) * tab_ref[2])


def _inproj_body(x_ref, sc_ref, sh_ref, w_ref, tab_ref, wpool_ref, pscale_ref,
                 qn_ref, wq_ref, kvn_ref, wkv_ref, wdw_ref, bdw_ref, cg_ref, cb_ref,
                 ya_ref, dq_ref, dk_ref, dvt_ref, iq_ref, ik_ref, iwt_ref,
                 qm_ref, km_ref, vmt_ref, yd_ref,
                 hbuf, zbuf, zal, *, tiles_per_seq, tm):
    i = pl.program_id(0)
    first = (i % tiles_per_seq) == 0

    @pl.when(first)
    def _():
        hbuf[0:POOL_HALO, :] = jnp.zeros((POOL_HALO, BRANCH), F32)
        zbuf[0:CONV_HALO, :] = jnp.zeros((CONV_HALO, BRANCH), F32)

    u = (_plain_norm(x_ref[...]) * (1.0 + sc_ref[...]) + sh_ref[...]).astype(BF16)

    def seg(a, b):
        return _mm(u, w_ref[:, a:b])

    hc = seg(OFF_CONV, IN_PAD)
    z = hc[:, :BRANCH] * jax.nn.sigmoid(hc[:, BRANCH:])
    zbuf[CONV_HALO:CONV_HALO + tm, :] = z
    acc = jnp.zeros((tm, BRANCH), F32) + bdw_ref[...]
    base = CONV_HALO - (CONV_WIDTH - 1)
    for r in range(SUBLANES):
        taps = [j for j in range(CONV_WIDTH) if (base + j) % SUBLANES == r]
        first = base + taps[0]
        rows = base + taps[-1] + tm - first
        zal[0:rows, :] = zbuf[first:first + rows, :]
        for j in taps:
            acc = acc + wdw_ref[j:j + 1, :] * zal[base + j - first:base + j - first + tm, :]
    zbuf[0:CONV_HALO, :] = zbuf[tm:tm + CONV_HALO, :]
    yd_ref[...] = _silu(_plain_norm(acc) * cg_ref[...] + cb_ref[...]).astype(BF16)

    h = seg(OFF_POOL, OFF_DQ)
    hbuf[POOL_HALO:POOL_HALO + tm, :] = h
    pos = (i % tiles_per_seq) * tm + lax.broadcasted_iota(I32, (tm, 1), 0)
    for g, w in enumerate(POOL_WINDOWS):
        lanes = slice(g * POOL_GROUP, (g + 1) * POOL_GROUP)
        acc = h[:, lanes]
        for j in range(1, w):
            acc = acc + hbuf[POOL_HALO - j:POOL_HALO - j + tm, lanes]
        cnt = jnp.minimum(pos + 1, w).astype(F32)
        pooled = acc / cnt - h[:, lanes]
        mixed = _mm(pooled.astype(BF16), wpool_ref[g])
        ya_ref[:, lanes] = (mixed * pscale_ref[:, lanes]).astype(BF16)
    hbuf[0:POOL_HALO, :] = hbuf[tm:tm + POOL_HALO, :]

    dq = seg(OFF_DQ, OFF_DK)
    for hd in range(DSA_HEADS):
        lanes = slice(hd * DSA_DIM, (hd + 1) * DSA_DIM)
        dq_ref[:, lanes] = (dq[:, lanes] * (DSA_DIM ** -0.5 / DSA_SLOPES[hd])).astype(BF16)
    dk_ref[...] = seg(OFF_DK, OFF_DV).astype(BF16)
    dvt_ref[0] = seg(OFF_DV, OFF_IQ).T.astype(BF16)
    iq_ref[...] = seg(OFF_IQ, OFF_IKW).astype(BF16)
    ikw = seg(OFF_IKW, OFF_CQ)
    ik_ref[...] = ikw.astype(BF16)
    iwt_ref[...] = ikw.T * (IDX_HEADS ** -0.5 * IDX_DIM ** -0.5)

    cq = seg(OFF_CQ, OFF_CKV)
    cqn = cq * lax.rsqrt(jnp.mean(cq * cq, axis=-1, keepdims=True) + LN_EPS) * qn_ref[...]
    q = _mm(cqn.astype(BF16), wq_ref[...])
    ckv = seg(OFF_CKV, OFF_KR)
    ckvn = ckv * lax.rsqrt(jnp.mean(ckv * ckv, axis=-1, keepdims=True) + LN_EPS) * kvn_ref[...]
    kv = _mm(ckvn.astype(BF16), wkv_ref[...])
    krope = _rope128(seg(OFF_KR, OFF_CONV), tab_ref).astype(BF16)
    qscale = (MLA_NOPE + MLA_ROPE) ** -0.5 * LOG2E
    vparts = []
    for hd in range(MLA_HEADS):
        b0 = hd * MLA_QK_PAD
        qm_ref[:, b0:b0 + 128] = (q[:, b0:b0 + 128] * qscale).astype(BF16)
        qm_ref[:, b0 + 128:b0 + 256] = (_rope128(q[:, b0 + 128:b0 + 256], tab_ref) * qscale).astype(BF16)
        km_ref[:, b0:b0 + 128] = kv[:, b0:b0 + 128].astype(BF16)
        km_ref[:, b0 + 128:b0 + 256] = krope
        vparts.append(kv[:, b0 + 128:b0 + 256])
    vmt_ref[0] = jnp.concatenate(vparts, axis=1).T.astype(BF16)


def _inproj(x2, sc, sh, w_in_p, rope_tab, w_pool, pool_scale, q_norm, w_q_up_p, kv_norm,
            w_kv_up, w_dw, b_dw, conv_g, conv_b, *, seq):
    n, d = x2.shape
    tm = min(TOKEN_TILE, seq)
    tiles_per_seq = seq // tm
    nt = n // tm
    row = lambda i: (i, 0)
    const2 = lambda i: (0, 0)
    const3 = lambda i: (0, 0, 0)
    per_batch = lambda i: (i // tiles_per_seq, 0, 0)
    once = dict(pipeline_mode=pl.Buffered(1))
    in_specs = [
        pl.BlockSpec((tm, d), row),
        pl.BlockSpec((None, 1, d), per_batch),
        pl.BlockSpec((None, 1, d), per_batch),
        pl.BlockSpec((d, IN_PAD), const2, **once),
        pl.BlockSpec((3, tm, 128), lambda i: (0, i % tiles_per_seq, 0)),
        pl.BlockSpec((4, POOL_GROUP, POOL_GROUP), const3, **once),
        pl.BlockSpec((1, BRANCH), const2, **once),
        pl.BlockSpec((1, Q_LORA), const2, **once),
        pl.BlockSpec((Q_LORA, MLA_HEADS * MLA_QK_PAD), const2, **once),
        pl.BlockSpec((1, KV_LORA), const2, **once),
        pl.BlockSpec((KV_LORA, MLA_HEADS * (MLA_NOPE + MLA_V)), const2, **once),
        pl.BlockSpec((CONV_WIDTH, BRANCH), const2, **once),
        pl.BlockSpec((1, BRANCH), const2, **once),
        pl.BlockSpec((1, BRANCH), const2, **once),
        pl.BlockSpec((1, BRANCH), const2, **once),
    ]
    out_shape = [
        jax.ShapeDtypeStruct((n, BRANCH), BF16),
        jax.ShapeDtypeStruct((n, BRANCH), BF16),
        jax.ShapeDtypeStruct((n, BRANCH), BF16),
        jax.ShapeDtypeStruct((nt, BRANCH, tm), BF16),
        jax.ShapeDtypeStruct((n, IDX_HEADS * IDX_DIM), BF16),
        jax.ShapeDtypeStruct((n, 128), BF16),
        jax.ShapeDtypeStruct((128, n), F32),
        jax.ShapeDtypeStruct((n, MLA_HEADS * MLA_QK_PAD), BF16),
        jax.ShapeDtypeStruct((n, MLA_HEADS * MLA_QK_PAD), BF16),
        jax.ShapeDtypeStruct((nt, BRANCH, tm), BF16),
        jax.ShapeDtypeStruct((n, BRANCH), BF16),
    ]
    out_specs = [
        pl.BlockSpec((tm, BRANCH), row),
        pl.BlockSpec((tm, BRANCH), row),
        pl.BlockSpec((tm, BRANCH), row),
        pl.BlockSpec((1, BRANCH, tm), lambda i: (i, 0, 0)),
        pl.BlockSpec((tm, IDX_HEADS * IDX_DIM), row),
        pl.BlockSpec((tm, 128), row),
        pl.BlockSpec((128, tm), lambda i: (0, i)),
        pl.BlockSpec((tm, MLA_HEADS * MLA_QK_PAD), row),
        pl.BlockSpec((tm, MLA_HEADS * MLA_QK_PAD), row),
        pl.BlockSpec((1, BRANCH, tm), lambda i: (i, 0, 0)),
        pl.BlockSpec((tm, BRANCH), row),
    ]
    return pl.pallas_call(
        functools.partial(_inproj_body, tiles_per_seq=tiles_per_seq, tm=tm),
        grid=(nt,),
        in_specs=in_specs,
        out_specs=out_specs,
        out_shape=out_shape,
        scratch_shapes=[pltpu.VMEM((POOL_HALO + tm, BRANCH), F32),
                        pltpu.VMEM((CONV_HALO + tm, BRANCH), F32),
                        pltpu.VMEM((CONV_HALO + tm, BRANCH), F32)],
        compiler_params=pltpu.CompilerParams(
            dimension_semantics=("arbitrary",), vmem_limit_bytes=VMEM_LIMIT),
    )(x2, sc, sh, w_in_p, rope_tab, w_pool, pool_scale, q_norm, w_q_up_p, kv_norm, w_kv_up,
      w_dw, b_dw, conv_g, conv_b)


def _attend_chunk(carry, logits, values, rates, l_ref, p_ref, acc_ref):
    m, s = carry
    heads = len(rates)
    m_new, s_new, alpha = [], [], []
    for hd in range(heads):
        x = logits(hd)
        l_ref[hd] = x
        m_new.append(jnp.maximum(m[hd], jnp.max(x, axis=0, keepdims=True)))
    for hd in range(heads):
        z = l_ref[hd] - m_new[hd]
        p = jnp.exp2(z * rates[hd]) if rates[hd] != 1.0 else jnp.exp2(z)
        p_ref[hd] = p.astype(BF16)
        a = jnp.exp2((m[hd] - m_new[hd]) * rates[hd])
        alpha.append(a)
        s_new.append(a * s[hd] + jnp.sum(p, axis=0, keepdims=True))
    for hd in range(heads):
        acc_ref[hd] = alpha[hd] * acc_ref[hd] + _mm(values(hd), p_ref[hd])
    return tuple(m_new), tuple(s_new)


def _query_rows(qt):
    t_row = pl.program_id(1) * qt + lax.broadcasted_iota(I32, (1, qt), 1)
    limit_row = (t_row // CHUNK + 1) * CHUNK
    return t_row, limit_row


def _finish_heads(o_ref, acc_ref, s, heads, width):
    for hd in range(heads):
        o = (acc_ref[hd] / s[hd]).T
        o_ref[:, hd * width:(hd + 1) * width] = o.astype(o_ref.dtype)


def _softmax_init(heads, qt):
    return (tuple(jnp.full((1, qt), M_INIT, F32) for _ in range(heads)),
            tuple(jnp.zeros((1, qt), F32) for _ in range(heads)))


def _f32_to_key(f):
    b = pltpu.bitcast(f, I32)
    return b ^ ((b >> 31) & 0x7FFFFFFF)


def _key_to_f32(k):
    return pltpu.bitcast(k ^ ((k >> 31) & 0x7FFFFFFF), F32)


def _dsa_body(q_ref, iq_ref, iwt_ref, ik_ref, k_ref, vt_ref, o_ref, sc_ref, acc_ref, l_ref, p_ref,
              *, n_sel, qt, kc):
    t0 = pl.program_id(1) * qt
    nck = (t0 + qt + kc - 1) // kc
    t_row, limit_row = _query_rows(qt)
    inf = float("inf")

    iqh = [iq_ref[:, hd * IDX_DIM:(hd + 1) * IDX_DIM] for hd in range(IDX_HEADS)]
    wrow = [iwt_ref[IDX_DIM + hd:IDX_DIM + hd + 1, :] for hd in range(IDX_HEADS)]

    def score_chunk(c, carry):
        mn, mx = carry
        r0 = pl.multiple_of(c * kc, kc)
        ikc = ik_ref[pl.ds(r0, kc), 0:IDX_DIM]
        sc = None
        for hd in range(IDX_HEADS):
            term = wrow[hd] * jnp.maximum(_nt(ikc, iqh[hd]), 0.0)
            sc = term if sc is None else sc + term
        adm = (r0 + lax.broadcasted_iota(I32, (kc, 1), 0)) < limit_row
        lo = jnp.where(adm, sc, -inf)
        sc_ref[pl.ds(r0, kc), :] = lo
        mx = jnp.maximum(mx, jnp.max(lo, axis=0, keepdims=True))
        mn = jnp.minimum(mn, jnp.min(jnp.where(adm, sc, inf), axis=0, keepdims=True))
        return mn, mx

    mn, mx = lax.fori_loop(0, nck, score_chunk,
                           (jnp.full((1, qt), inf, F32), jnp.full((1, qt), -inf, F32)))

    def count_where(pred):
        def body(c, acc):
            r0 = pl.multiple_of(c * kc, kc)
            hit = jnp.where(pred(sc_ref[pl.ds(r0, kc), :], r0), 1.0, 0.0)
            return acc + jnp.sum(hit.reshape(kc // COUNT_ROWS, COUNT_ROWS, qt), axis=0)
        acc = lax.fori_loop(0, nck, body, jnp.zeros((COUNT_ROWS, qt), F32))
        return jnp.sum(acc, axis=0, keepdims=True)

    k_f = float(n_sel)

    def is_active(st):
        lo_k, hi_k, cnt_lo, _ = st
        return (cnt_lo > k_f) & (hi_k > lo_k + 1)

    def any_lane(mask):
        return jnp.max(jnp.where(mask, 1, 0).astype(I32))

    def search_step(it, st):
        lo_k, hi_k, cnt_lo, cnt_hi = st
        active = is_active(st)
        mid_val = _f32_to_key(0.5 * _key_to_f32(lo_k) + 0.5 * _key_to_f32(hi_k))
        mid_key = (lo_k >> 1) + (hi_k >> 1) + (lo_k & hi_k & 1)
        trial_k = jnp.where(it < VALUE_BISECT_STEPS, mid_val, mid_key)
        trial_k = jnp.minimum(jnp.maximum(trial_k, lo_k + 1), hi_k - 1)
        trial_f = _key_to_f32(trial_k)
        cnt = count_where(lambda s, r0: s >= trial_f)
        ge = cnt >= k_f
        up_lo, up_hi = active & ge, active & jnp.logical_not(ge)
        lo_k = jnp.where(up_lo, trial_k, lo_k)
        cnt_lo = jnp.where(up_lo, cnt, cnt_lo)
        hi_k = jnp.where(up_hi, trial_k, hi_k)
        cnt_hi = jnp.where(up_hi, cnt, cnt_hi)
        return lo_k, hi_k, cnt_lo, cnt_hi

    st = (_f32_to_key(mn), _f32_to_key(mx) + 1, limit_row.astype(F32), jnp.zeros((1, qt), F32))
    go = any_lane(is_active(st))
    n_first = jnp.where(go > 0, SEARCH_FIRST_STEPS, 0)
    st = lax.fori_loop(0, n_first, search_step, st)

    def search_cond(carry):
        return (carry[1] > 0) & (carry[0] < SEARCH_MAX_STEPS)

    def search_pair(carry):
        it, _, st = carry
        st = search_step(it + 1, search_step(it, st))
        return it + 2, any_lane(is_active(st)), st

    _, _, st = lax.while_loop(search_cond, search_pair, (n_first, any_lane(is_active(st)), st))
    lo_k, hi_k, cnt_lo, cnt_hi = st
    thr = _key_to_f32(lo_k)

    tie = cnt_lo > k_f
    need = k_f - cnt_hi

    @pl.when(any_lane(tie) > 0)
    def _():
        def idx_step(it, cut):
            trial = cut | jnp.left_shift(jnp.int32(1), 30 - it)
            cnt = count_where(lambda s, r0: (s == thr) & (
                (r0 + lax.broadcasted_iota(I32, (kc, 1), 0)) < trial))
            return jnp.where(cnt < need, trial, cut)
        cut = lax.fori_loop(0, 31, idx_step, jnp.zeros((1, qt), I32))

        def drop(c, carry):
            r0 = pl.multiple_of(c * kc, kc)
            s = sc_ref[pl.ds(r0, kc), :]
            kpos = r0 + lax.broadcasted_iota(I32, (kc, 1), 0)
            sc_ref[pl.ds(r0, kc), :] = jnp.where(tie & (s == thr) & (kpos > cut), -inf, s)
            return carry
        lax.fori_loop(0, nck, drop, 0)

    qh = [q_ref[:, hd * DSA_DIM:(hd + 1) * DSA_DIM] for hd in range(DSA_HEADS)]
    acc_ref[...] = jnp.zeros_like(acc_ref)

    rates = tuple(slope * LOG2E for slope in DSA_SLOPES)

    def attn_chunk(c, carry):
        r0 = pl.multiple_of(c * kc, kc)
        sel = sc_ref[pl.ds(r0, kc), :] >= thr
        dist = jnp.abs(t_row - (r0 + lax.broadcasted_iota(I32, (kc, 1), 0))).astype(F32)
        lanes = lambda hd: slice(hd * DSA_DIM, (hd + 1) * DSA_DIM)
        logits = lambda hd: jnp.where(
            sel, _nt(k_ref[pl.ds(r0, kc), lanes(hd)], qh[hd]) - dist, NEG)
        values = lambda hd: vt_ref[c, lanes(hd), :]
        return _attend_chunk(carry, logits, values, rates, l_ref, p_ref, acc_ref)

    _, s = lax.fori_loop(0, nck, attn_chunk, _softmax_init(DSA_HEADS, qt))
    _finish_heads(o_ref, acc_ref, s, DSA_HEADS, DSA_DIM)


def _dsa(q, k, vt, iq, ik, iwt, *, batch, seq):
    n = q.shape[0]
    qt, kc = min(Q_TILE, seq), min(KEY_CHUNK, seq)
    nqb = seq // qt
    n_sel = min(TOPK_MAX, seq // 4)
    qrow = lambda b, i: (b * nqb + i, 0)
    return pl.pallas_call(
        functools.partial(_dsa_body, n_sel=n_sel, qt=qt, kc=kc),
        grid=(batch, nqb),
        in_specs=[
            pl.BlockSpec((qt, BRANCH), qrow),
            pl.BlockSpec((qt, IDX_HEADS * IDX_DIM), qrow),
            pl.BlockSpec((128, qt), lambda b, i: (0, b * nqb + i)),
            pl.BlockSpec((seq, 128), lambda b, i: (b, 0)),
            pl.BlockSpec((seq, BRANCH), lambda b, i: (b, 0)),
            pl.BlockSpec((seq // kc, BRANCH, kc), lambda b, i: (b, 0, 0)),
        ],
        out_specs=pl.BlockSpec((qt, BRANCH), qrow),
        out_shape=jax.ShapeDtypeStruct((n, BRANCH), BF16),
        scratch_shapes=[pltpu.VMEM((seq, qt), F32),
                        pltpu.VMEM((DSA_HEADS, DSA_DIM, qt), F32),
                        pltpu.VMEM((DSA_HEADS, kc, qt), F32),
                        pltpu.VMEM((DSA_HEADS, kc, qt), BF16)],
        compiler_params=pltpu.CompilerParams(
            dimension_semantics=("arbitrary", "arbitrary"), vmem_limit_bytes=VMEM_LIMIT),
    )(q, iq, iwt, ik, k, vt)


def _mla_body(q_ref, k_ref, vt_ref, o_ref, acc_ref, l_ref, p_ref, *, qt, kc):
    t0 = pl.program_id(1) * qt
    n_full = t0 // kc
    nck = (t0 + qt + kc - 1) // kc
    _, limit_row = _query_rows(qt)
    qh = [q_ref[:, hd * MLA_QK_PAD:(hd + 1) * MLA_QK_PAD] for hd in range(MLA_HEADS)]
    acc_ref[...] = jnp.zeros_like(acc_ref)
    rates = (1.0,) * MLA_HEADS

    def attn_chunk(masked, c, carry):
        r0 = pl.multiple_of(c * kc, kc)

        def logits(hd):
            x = _nt(k_ref[pl.ds(r0, kc), hd * MLA_QK_PAD:(hd + 1) * MLA_QK_PAD], qh[hd])
            if masked:
                x = jnp.where((r0 + lax.broadcasted_iota(I32, (kc, 1), 0)) < limit_row, x, NEG)
            return x

        values = lambda hd: vt_ref[c, hd * MLA_V:(hd + 1) * MLA_V, :]
        return _attend_chunk(carry, logits, values, rates, l_ref, p_ref, acc_ref)

    carry = lax.fori_loop(0, n_full, functools.partial(attn_chunk, False),
                          _softmax_init(MLA_HEADS, qt))
    _, s = lax.fori_loop(n_full, nck, functools.partial(attn_chunk, True), carry)
    _finish_heads(o_ref, acc_ref, s, MLA_HEADS, MLA_V)


def _mla(q, k, vt, *, batch, seq):
    n = q.shape[0]
    qt, kc = min(Q_TILE, seq), min(KEY_CHUNK, seq)
    nqb = seq // qt
    width = MLA_HEADS * MLA_QK_PAD
    return pl.pallas_call(
        functools.partial(_mla_body, qt=qt, kc=kc),
        grid=(batch, nqb),
        in_specs=[
            pl.BlockSpec((qt, width), lambda b, i: (b * nqb + i, 0)),
            pl.BlockSpec((seq, width), lambda b, i: (b, 0)),
            pl.BlockSpec((seq // kc, BRANCH, kc), lambda b, i: (b, 0, 0)),
        ],
        out_specs=pl.BlockSpec((qt, BRANCH), lambda b, i: (b * nqb + i, 0)),
        out_shape=jax.ShapeDtypeStruct((n, BRANCH), BF16),
        scratch_shapes=[pltpu.VMEM((MLA_HEADS, MLA_V, qt), F32),
                        pltpu.VMEM((MLA_HEADS, kc, qt), F32),
                        pltpu.VMEM((MLA_HEADS, kc, qt), BF16)],
        compiler_params=pltpu.CompilerParams(
            dimension_semantics=("arbitrary", "arbitrary"), vmem_limit_bytes=VMEM_LIMIT),
    )(q, k, vt)


def _merge_body(x_ref, sc_ref, sh_ref, ya_ref, yb_ref, yc_ref, yd_ref, wg_ref, bg_ref, wb_ref,
                o_ref, u_ref):
    @pl.when(pl.program_id(1) == 0)
    def _():
        u_ref[...] = (_plain_norm(x_ref[...]) * (1.0 + sc_ref[...]) + sh_ref[...]).astype(BF16)

    u = u_ref[...]
    merged = None
    for br, y_ref in enumerate((ya_ref, yb_ref, yc_ref, yd_ref)):
        gate = jax.nn.sigmoid(_mm(u, wg_ref[br]) + bg_ref[br])
        term = gate * _mm(y_ref[...], wb_ref[br])
        merged = term if merged is None else merged + term
    o_ref[...] = merged.astype(BF16)


def _merge(x2, sc, sh, ys, w_gate, b_gate, w_branch, *, seq):
    n, d = x2.shape
    tm = min(TOKEN_TILE, seq)
    tn = 512
    tiles_per_seq = seq // tm
    per_batch = lambda i, j: (i // tiles_per_seq, 0, 0)
    row = lambda i, j: (i, 0)
    return pl.pallas_call(
        _merge_body,
        grid=(n // tm, d // tn),
        in_specs=[
            pl.BlockSpec((tm, d), row),
            pl.BlockSpec((None, 1, d), per_batch),
            pl.BlockSpec((None, 1, d), per_batch),
            pl.BlockSpec((tm, BRANCH), row),
            pl.BlockSpec((tm, BRANCH), row),
            pl.BlockSpec((tm, BRANCH), row),
            pl.BlockSpec((tm, BRANCH), row),
            pl.BlockSpec((4, d, tn), lambda i, j: (0, 0, j)),
            pl.BlockSpec((4, 1, tn), lambda i, j: (0, 0, j)),
            pl.BlockSpec((4, BRANCH, tn), lambda i, j: (0, 0, j)),
        ],
        out_specs=pl.BlockSpec((tm, tn), lambda i, j: (i, j)),
        out_shape=jax.ShapeDtypeStruct((n, d), BF16),
        scratch_shapes=[pltpu.VMEM((tm, d), BF16)],
        compiler_params=pltpu.CompilerParams(
            dimension_semantics=("arbitrary", "arbitrary"), vmem_limit_bytes=VMEM_LIMIT),
    )(x2, sc, sh, *ys, w_gate, b_gate, w_branch)


def _deepnorm_ln(x, gate, upd, g, b):
    y = ALPHA * x + (1.0 + gate) * upd
    return _plain_norm(y) * g + b


def _oproj_body(x_ref, m_ref, g1_ref, wo_ref, lg_ref, lb_ref, o_ref):
    upd = _mm(m_ref[...], wo_ref[...])
    o_ref[...] = _deepnorm_ln(x_ref[...], g1_ref[...], upd, lg_ref[...], lb_ref[...])


def _oproj(x2, merged, g1, w_o, ln_g, ln_b, *, seq):
    n, d = x2.shape
    tm = min(TOKEN_TILE, seq)
    tiles_per_seq = seq // tm
    row = lambda i: (i, 0)
    const = lambda i: (0, 0)
    once = dict(pipeline_mode=pl.Buffered(1))
    return pl.pallas_call(
        _oproj_body,
        grid=(n // tm,),
        in_specs=[
            pl.BlockSpec((tm, d), row),
            pl.BlockSpec((tm, d), row),
            pl.BlockSpec((None, 1, d), lambda i: (i // tiles_per_seq, 0, 0)),
            pl.BlockSpec((d, d), const, **once),
            pl.BlockSpec((1, d), const, **once),
            pl.BlockSpec((1, d), const, **once),
        ],
        out_specs=pl.BlockSpec((tm, d), row),
        out_shape=jax.ShapeDtypeStruct((n, d), F32),
        compiler_params=pltpu.CompilerParams(
            dimension_semantics=("arbitrary",), vmem_limit_bytes=VMEM_LIMIT),
    )(x2, merged, g1, w_o, ln_g, ln_b)


def _ffn_body(x_ref, sc_ref, sh_ref, g2_ref, wa_ref, wg_ref, wo_ref, lg_ref, lb_ref,
              o_ref, u_ref, acc_ref):
    j = pl.program_id(1)

    @pl.when(j == 0)
    def _():
        u_ref[...] = (_plain_norm(x_ref[...]) * (1.0 + sc_ref[...]) + sh_ref[...]).astype(BF16)
        acc_ref[...] = jnp.zeros_like(acc_ref)

    u = u_ref[...]
    hid = _silu(_mm(u, wa_ref[...])) * _mm(u, wg_ref[...])
    acc_ref[...] += _mm(hid.astype(BF16), wo_ref[...])

    @pl.when(j == pl.num_programs(1) - 1)
    def _():
        o_ref[...] = _deepnorm_ln(x_ref[...], g2_ref[...], acc_ref[...], lg_ref[...], lb_ref[...])


def _ffn(x2, sc, sh, g2, w_ffn_in, w_ffn_out, ln_g, ln_b, *, seq):
    n, d = x2.shape
    tm = min(TOKEN_TILE, seq)
    th = 512
    nh = FFN_HIDDEN // th
    tiles_per_seq = seq // tm
    per_batch = lambda i, j: (i // tiles_per_seq, 0, 0)
    row = lambda i, j: (i, 0)
    const = lambda i, j: (0, 0)
    return pl.pallas_call(
        _ffn_body,
        grid=(n // tm, nh),
        in_specs=[
            pl.BlockSpec((tm, d), row),
            pl.BlockSpec((None, 1, d), per_batch),
            pl.BlockSpec((None, 1, d), per_batch),
            pl.BlockSpec((None, 1, d), per_batch),
            pl.BlockSpec((d, th), lambda i, j: (0, j)),
            pl.BlockSpec((d, th), lambda i, j: (0, nh + j)),
            pl.BlockSpec((th, d), lambda i, j: (j, 0)),
            pl.BlockSpec((1, d), const),
            pl.BlockSpec((1, d), const),
        ],
        out_specs=pl.BlockSpec((tm, d), row),
        out_shape=jax.ShapeDtypeStruct((n, d), F32),
        scratch_shapes=[pltpu.VMEM((tm, d), BF16), pltpu.VMEM((tm, d), F32)],
        compiler_params=pltpu.CompilerParams(
            dimension_semantics=("arbitrary", "arbitrary"), vmem_limit_bytes=VMEM_LIMIT),
    )(x2, sc, sh, g2, w_ffn_in, w_ffn_in, w_ffn_out, ln_g, ln_b)


def _pad_w_in(w):
    d = w.shape[0]
    z = lambda k: jnp.zeros((d, k), w.dtype)
    return jnp.concatenate([w[:, :3152], z(48), w[:, 3152:3856], z(64), w[:, 3856:]],
                           axis=1).astype(BF16)


def _pad_w_q_up(w):
    per = MLA_NOPE + MLA_ROPE
    w4 = w.reshape(Q_LORA, MLA_HEADS, per)
    w4 = jnp.pad(w4, ((0, 0), (0, 0), (0, MLA_QK_PAD - per)))
    return w4.reshape(Q_LORA, MLA_HEADS * MLA_QK_PAD).astype(BF16)


def _rope_tables(seq):
    pos = jnp.arange(seq, dtype=F32)
    inv_freq = ROPE_BASE ** (-jnp.arange(0, MLA_ROPE, 2, dtype=F32) / MLA_ROPE)
    ang = pos[:, None] * inv_freq[None, :]
    cos, sin = jnp.cos(ang), jnp.sin(ang)
    z32 = jnp.zeros_like(cos)
    z64 = jnp.zeros((seq, 64), F32)
    return jnp.stack([
        jnp.concatenate([cos, cos, z64], axis=1),
        jnp.concatenate([z32, sin, z64], axis=1),
        jnp.concatenate([-sin, z32, z64], axis=1)])


def kernel(x, c, w_ada, b_ada, w_in, w_pool, pool_scale, q_norm, w_q_up, kv_norm, w_kv_up,
           w_dw, b_dw, conv_ln_g, conv_ln_b, w_branch, w_gate, b_gate, w_o, ln1_g, ln1_b,
           w_ffn_in, w_ffn_out, ln2_g, ln2_b):
    batch, seq, d = x.shape
    n = batch * seq
    depth = w_ada.shape[0]
    assert seq % min(TOKEN_TILE, seq) == 0 and seq % min(KEY_CHUNK, seq) == 0
    assert min(TOKEN_TILE, seq) == min(KEY_CHUNK, seq) and batch <= 8

    c_pad = jnp.zeros((8, d), F32).at[:batch].set(c)
    mod = _ada(c_pad, w_ada, b_ada)[:, :batch].reshape(depth, batch, 6, 1, d)
    rope_tab = _rope_tables(seq)
    x2 = x.reshape(n, d)
    row = lambda v: v.reshape(1, -1)

    for l in range(depth):
        sh1, sc1, g1, sh2, sc2, g2 = (mod[l, :, k] for k in range(6))
        (y_a, dq, dk, dvt, iq, ik, iwt, qm, km, vmt, y_d) = _inproj(
            x2, sc1, sh1, _pad_w_in(w_in[l]), rope_tab, w_pool[l].astype(BF16),
            row(pool_scale[l]), row(q_norm[l]), _pad_w_q_up(w_q_up[l]), row(kv_norm[l]),
            w_kv_up[l].astype(BF16), w_dw[l], row(b_dw[l]), row(conv_ln_g[l]),
            row(conv_ln_b[l]), seq=seq)
        y_b = _dsa(dq, dk, dvt, iq, ik, iwt, batch=batch, seq=seq)
        y_c = _mla(qm, km, vmt, batch=batch, seq=seq)
        merged = _merge(x2, sc1, sh1, (y_a, y_b, y_c, y_d), w_gate[l].astype(BF16),
                        b_gate[l].reshape(4, 1, d), w_branch[l].astype(BF16), seq=seq)
        x2 = _oproj(x2, merged, g1, w_o[l].astype(BF16), row(ln1_g[l]), row(ln1_b[l]), seq=seq)
        x2 = _ffn(x2, sc2, sh2, g2, w_ffn_in[l].astype(BF16), w_ffn_out[l].astype(BF16),
                  row(ln2_g[l]), row(ln2_b[l]), seq=seq)
    return x2.reshape(batch, seq, d)
```
